```python
import jax, jax.numpy as jnp
from jax import lax
import numpy as np

D_MODEL = 4096
BATCH = 8
SEQ = 2048
DEPTH = 1
DEC_BATCH = 1
DEC_SEQ = 8192
PAST_LEN = 128

HEAD_DIM = 128
N_HEADS = D_MODEL // HEAD_DIM
N_KV_HEADS = N_HEADS // 4
GQA_GROUP = N_HEADS // N_KV_HEADS
Q_DIM = N_HEADS * HEAD_DIM
KV_DIM = N_KV_HEADS * HEAD_DIM
CONV_DIM = D_MODEL
CONV_WIDTH = 3
D_FF = 11008
GRID_W = 64
Q_BLOCK = 128
ROPE_THETA = 10000.0
AXIAL_DIM = HEAD_DIM // 2
NORM_EPS = 1e-6
IN_SIZES = [Q_DIM, KV_DIM, KV_DIM, CONV_DIM, CONV_DIM, CONV_DIM, D_MODEL, D_MODEL]
IN_COLS = int(sum(IN_SIZES))
IN_SPLITS = [int(v) for v in np.cumsum(IN_SIZES)[:-1]]

kernel_name = 'hybrid_gqa_shortconv_convffn_encoder'


def rmsnorm(x, g):
    xf = x.astype(jnp.float32)
    y = xf * lax.rsqrt(jnp.mean(xf * xf, axis=-1, keepdims=True) + NORM_EPS)
    return (y * g.astype(jnp.float32)).astype(x.dtype)


def dwconv3(x, w, b):
    xp = jnp.pad(x, ((0, 0), (1, 1), (0, 0)))
    return xp[:, :-2] * w[0] + xp[:, 1:-1] * w[1] + xp[:, 2:] * w[2] + b


def axial_angles(seq_len, dtype):
    n_rows = seq_len // GRID_W
    rows = jnp.repeat(jnp.arange(n_rows, dtype=jnp.int32), GRID_W)
    cols = jnp.arange(seq_len, dtype=jnp.int32) - rows * GRID_W
    inv_freq = ROPE_THETA ** (-jnp.arange(0, AXIAL_DIM, 2, dtype=jnp.float32) / AXIAL_DIM)
    ang_r = rows.astype(jnp.float32)[:, None] * inv_freq[None, :]
    ang_c = cols.astype(jnp.float32)[:, None] * inv_freq[None, :]
    f = lambda a: (jnp.cos(a)[:, None, :].astype(dtype), jnp.sin(a)[:, None, :].astype(dtype))
    return f(ang_r), f(ang_c)


def rope_half(x, cos, sin):
    h = x.shape[-1] // 2
    x1, x2 = x[..., :h], x[..., h:]
    return jnp.concatenate([x1 * cos - x2 * sin, x2 * cos + x1 * sin], axis=-1)


def axial_rope(x, rc, cc):
    return jnp.concatenate([rope_half(x[..., :AXIAL_DIM], *rc), rope_half(x[..., AXIAL_DIM:], *cc)], axis=-1)


def block_attention(q, k, v):
    b, s = q.shape[0], q.shape[1]
    nb = s // Q_BLOCK
    qb = q.reshape(b, nb, Q_BLOCK, N_KV_HEADS, GQA_GROUP, HEAD_DIM).transpose(1, 0, 2, 3, 4, 5)
    scale = HEAD_DIM ** -0.5

    def one_block(q_blk):
        sc = jnp.einsum('bqkgd,bskd->bkgqs', q_blk, k).astype(jnp.float32) * scale
        p = jax.nn.softmax(sc, axis=-1).astype(v.dtype)
        return jnp.einsum('bkgqs,bskd->bqkgd', p, v)

    out = lax.map(one_block, qb)
    return out.transpose(1, 0, 2, 3, 4, 5).reshape(b, s, Q_DIM)


def encoder_layer(h, c, w_ada, b_ada, g_mix_pre, w_in, g_q, g_k, conv_w, conv_b, w_o, g_mix_post,
                  g_ffn_pre, w_up, ffn_conv_w, ffn_conv_b, w_down, g_ffn_post):
    b, s, _ = h.shape
    mod = jnp.einsum('bd,de->be', jax.nn.silu(c), w_ada) + b_ada
    sh1, sc1, gt1, sh2, sc2, gt2 = jnp.split(mod[:, None, :], 6, axis=-1)

    u = rmsnorm(h, g_mix_pre) * (1 + sc1) + sh1
    proj = jnp.einsum('bsd,de->bse', u, w_in)
    q, k, v, gb, gc, xin, ga_attn, ga_conv = jnp.split(proj, IN_SPLITS, axis=-1)
    q = rmsnorm(q.reshape(b, s, N_HEADS, HEAD_DIM), g_q)
    k = rmsnorm(k.reshape(b, s, N_KV_HEADS, HEAD_DIM), g_k)
    v = v.reshape(b, s, N_KV_HEADS, HEAD_DIM)
    rc, cc = axial_angles(s, h.dtype)
    q = axial_rope(q, rc, cc)
    k = axial_rope(k, rc, cc)
    attn = block_attention(q, k, v)
    conv = gb * dwconv3(gc * xin, conv_w, conv_b)
    merged = jax.nn.sigmoid(ga_attn) * attn + jax.nn.sigmoid(ga_conv) * conv
    out = jnp.einsum('bsd,de->bse', merged, w_o)
    h = h + gt1 * rmsnorm(out, g_mix_post)

    u2 = rmsnorm(h, g_ffn_pre) * (1 + sc2) + sh2
    z = dwconv3(jnp.einsum('bsd,df->bsf', u2, w_up), ffn_conv_w, ffn_conv_b)
    za, zb = jnp.split(z, 2, axis=-1)
    y = jnp.einsum('bsf,fd->bsd', jax.nn.silu(za) * zb, w_down)
    h = h + gt2 * rmsnorm(y, g_ffn_post)
    return h


def setup_inputs(seed: int = 0) -> dict:
    key = jax.random.key(seed)
    ks = jax.random.split(key, 20)
    nrm = lambda k, shape, s: jax.random.normal(k, shape, jnp.float32) * s
    gain = lambda k, shape: 1.0 + 0.05 * jax.random.normal(k, shape, jnp.float32)
    return {
        'x_prompt': nrm(ks[0], (BATCH, SEQ, D_MODEL), 1.0),
        'x_sample': nrm(ks[1], (DEC_BATCH, DEC_SEQ, D_MODEL), 1.0),
        'c_prompt': nrm(ks[2], (BATCH, D_MODEL), 1.0),
        'c_sample': nrm(ks[3], (DEC_BATCH, D_MODEL), 1.0),
        'w_ada': nrm(ks[4], (DEPTH, D_MODEL, 6 * D_MODEL), D_MODEL ** -0.5),
        'b_ada': nrm(ks[5], (DEPTH, 6 * D_MODEL), 0.02),
        'g_mix_pre': gain(ks[6], (DEPTH, D_MODEL)),
        'w_in': nrm(ks[7], (DEPTH, D_MODEL, IN_COLS), D_MODEL ** -0.5),
        'g_q': gain(ks[8], (DEPTH, HEAD_DIM)),
        'g_k': gain(ks[9], (DEPTH, HEAD_DIM)),
        'conv_w': nrm(ks[10], (DEPTH, CONV_WIDTH, CONV_DIM), CONV_WIDTH ** -0.5),
        'conv_b': nrm(ks[11], (DEPTH, CONV_DIM), 0.02),
        'w_o': nrm(ks[12], (DEPTH, D_MODEL, D_MODEL), D_MODEL ** -0.5),
        'g_mix_post': gain(ks[13], (DEPTH, D_MODEL)),
        'g_ffn_pre': gain(ks[14], (DEPTH, D_MODEL)),
        'w_up': nrm(ks[15], (DEPTH, D_MODEL, 2 * D_FF), D_MODEL ** -0.5),
        'ffn_conv_w': nrm(ks[16], (DEPTH, CONV_WIDTH, 2 * D_FF), CONV_WIDTH ** -0.5),
        'ffn_conv_b': nrm(ks[17], (DEPTH, 2 * D_FF), 0.02),
        'w_down': nrm(ks[18], (DEPTH, D_FF, D_MODEL), D_FF ** -0.5),
        'g_ffn_post': gain(ks[19], (DEPTH, D_MODEL)),
    }


def reference(x_prompt, x_sample, c_prompt, c_sample, w_ada, b_ada, g_mix_pre, w_in, g_q, g_k,
              conv_w, conv_b, w_o, g_mix_post, g_ffn_pre, w_up, ffn_conv_w, ffn_conv_b, w_down,
              g_ffn_post):
    hp = x_prompt
    hs = x_sample
    for l in range(DEPTH):
        p = (w_ada[l], b_ada[l], g_mix_pre[l], w_in[l], g_q[l], g_k[l], conv_w[l], conv_b[l], w_o[l],
             g_mix_post[l], g_ffn_pre[l], w_up[l], ffn_conv_w[l], ffn_conv_b[l], w_down[l], g_ffn_post[l])
        hp = encoder_layer(hp, c_prompt, *p)
        hs = encoder_layer(hs, c_sample, *p)
    y_prompt = hp
    y_sample = hs
    return (y_prompt, y_sample)
```

```python
import functools
import math

import numpy as np
import jax
import jax.numpy as jnp
from jax import lax
from jax.experimental import pallas as pl
from jax.experimental.pallas import tpu as pltpu

HEAD_DIM = 128
GQA_GROUP = 4
GRID_W = 64
ROPE_THETA = 10000.0
AXIAL_DIM = HEAD_DIM // 2
NORM_EPS = 1e-6
LOG2E = 1.4426950408889634

BF16_ROWS = 16
V7X_VMEM_BUDGET = 56 * 1024 * 1024

f32 = jnp.float32
bf16 = jnp.bfloat16


def _params(sem, vmem=None):
    return pltpu.CompilerParams(dimension_semantics=sem, vmem_limit_bytes=vmem)


def _rms(x):
    return x * lax.rsqrt(jnp.mean(x * x, axis=-1, keepdims=True) + NORM_EPS)


def _ada_kernel(c_ref, w_ref, b_ref, o_ref):
    c = c_ref[...]
    a = (c * jax.nn.sigmoid(c)).astype(bf16)
    o_ref[...] = jnp.dot(a, w_ref[...].astype(bf16), preferred_element_type=f32) + b_ref[...]


def _ada(c_pad, w_ada, b_ada):
    rows, d = c_pad.shape
    n = w_ada.shape[1]
    tn = 512
    return pl.pallas_call(
        _ada_kernel,
        out_shape=jax.ShapeDtypeStruct((rows, n), f32),
        grid=(n // tn,),
        in_specs=[pl.BlockSpec((rows, d), lambda j: (0, 0)),
                  pl.BlockSpec((d, tn), lambda j: (0, j)),
                  pl.BlockSpec((1, tn), lambda j: (0, j))],
        out_specs=pl.BlockSpec((rows, tn), lambda j: (0, j)),
        compiler_params=_params(("parallel",), 40 * 1024 * 1024),
        name="ada_mod",
    )(c_pad, w_ada, b_ada.reshape(1, n))


class _Geom:
    def __init__(self, b, s, bs, ss):
        assert ss % s == 0
        self.b, self.s, self.bs, self.ss = b, s, bs, ss
        self.spc = ss // s
        self.n_prompt_chunks = b
        self.n_chunks = b + bs * self.spc
        self.t_prompt = b * s
        self.t = b * s + bs * ss

    def mod_row(self, c):
        return jnp.where(c < self.b, c, self.b + (c - self.b) // self.spc)


def _x_specs(g, tm, d):
    nj = g.s // tm
    nb = g.b

    def xp_map(c, j):
        return (jnp.minimum(c, nb - 1), jnp.where(c < nb, j, nj - 1), 0)

    def xs_map(c, j):
        r = jnp.maximum(c - nb, 0)
        return (r // g.spc, jnp.where(c < nb, 0, (r % g.spc) * nj + j), 0)

    return [pl.BlockSpec((1, tm, d), xp_map), pl.BlockSpec((1, tm, d), xs_map)]


def _mod_spec(g, d, col):
    return pl.BlockSpec((1, 1, d), lambda c, j: (g.mod_row(c), 0, col))


def _prenorm_kernel(xp_ref, xs_ref, g_ref, sh_ref, sc_ref, o_ref, *, nb):
    c = pl.program_id(0)

    def compute(x):
        y = _rms(x) * g_ref[...]
        o_ref[...] = (y * (1.0 + sc_ref[0]) + sh_ref[0]).astype(o_ref.dtype)

    @pl.when(c < nb)
    def _():
        compute(xp_ref[0])

    @pl.when(c >= nb)
    def _():
        compute(xs_ref[0])


def _prenorm(g, xp, xs, gain, mod3, sh_col, sc_col):
    d = xp.shape[-1]
    tm = min(512, g.s)
    nj = g.s // tm
    return pl.pallas_call(
        functools.partial(_prenorm_kernel, nb=g.b),
        out_shape=jax.ShapeDtypeStruct((g.t, d), bf16),
        grid=(g.n_chunks, nj),
        in_specs=_x_specs(g, tm, d) + [
            pl.BlockSpec((1, d), lambda c, j: (0, 0)),
            _mod_spec(g, d, sh_col), _mod_spec(g, d, sc_col)],
        out_specs=pl.BlockSpec((tm, d), lambda c, j: (c * nj + j, 0)),
        compiler_params=_params(("parallel", "parallel"), 48 * 1024 * 1024),
        name="prenorm",
    )(xp, xs, gain.reshape(1, d), mod3, mod3)


def _mm_kernel(a_ref, b_ref, o_ref):
    o_ref[...] = jnp.dot(a_ref[...], b_ref[...], preferred_element_type=f32).astype(o_ref.dtype)


def _matmul(a, b, out_dtype, tm, tn, name):
    m, k = a.shape
    n = b.shape[1]
    assert m % tm == 0 and n % tn == 0
    return pl.pallas_call(
        _mm_kernel,
        out_shape=jax.ShapeDtypeStruct((m, n), out_dtype),
        grid=(m // tm, n // tn),
        in_specs=[pl.BlockSpec((tm, k), lambda i, j: (i, 0)),
                  pl.BlockSpec((k, tn), lambda i, j: (0, j))],
        out_specs=pl.BlockSpec((tm, tn), lambda i, j: (i, j)),
        compiler_params=_params(("parallel", "parallel"), V7X_VMEM_BUDGET),
        name=name,
    )(a, b)


def _qkprep_kernel(q_ref, k_ref, v_ref, gq_ref, gk_ref, c_ref, s1_ref, s2_ref,
                   qo_ref, ko_ref, vo_ref, *, tq, scale):
    cos = c_ref[...]
    s1 = s1_ref[...]
    s2 = s2_ref[...]
    hd = HEAD_DIM

    def norm_rope(x, gain):
        y = _rms(x.astype(f32)) * gain
        return (y * cos + pltpu.roll(y, hd - AXIAL_DIM // 2, 1) * s1
                + pltpu.roll(y, AXIAL_DIM // 2, 1) * s2)

    n_sub = q_ref.shape[0] // tq
    for gi in range(GQA_GROUP):
        y = norm_rope(q_ref[:, gi * hd:(gi + 1) * hd], gq_ref[...]) * scale
        for sub in range(n_sub):
            qo_ref[0, sub, gi] = y[sub * tq:(sub + 1) * tq].astype(qo_ref.dtype)
    ko_ref[0] = norm_rope(k_ref[...], gk_ref[...]).astype(ko_ref.dtype)
    vo_ref[0] = v_ref[...]


def _rope_tables(n_pos):
    t = jnp.arange(n_pos, dtype=jnp.int32)
    rows = t // GRID_W
    cols = t - rows * GRID_W
    inv_freq = ROPE_THETA ** (-jnp.arange(0, AXIAL_DIM, 2, dtype=f32) / AXIAL_DIM)
    ang_r = rows.astype(f32)[:, None] * inv_freq[None, :]
    ang_c = cols.astype(f32)[:, None] * inv_freq[None, :]
    z = jnp.zeros_like(ang_r)
    cos = jnp.concatenate([jnp.cos(ang_r), jnp.cos(ang_r), jnp.cos(ang_c), jnp.cos(ang_c)], axis=-1)
    s1 = jnp.concatenate([-jnp.sin(ang_r), z, -jnp.sin(ang_c), z], axis=-1)
    s2 = jnp.concatenate([z, jnp.sin(ang_r), z, jnp.sin(ang_c)], axis=-1)
    return cos, s1, s2


def _qkprep(g, proj, g_q, g_k, n_kv, tq):
    t = g.t
    hd = HEAD_DIM
    tt = min(1024, g.s)
    assert tt % tq == 0
    n_sub = tt // tq
    qw = GQA_GROUP * hd
    k_blk0 = (n_kv * qw) // hd
    v_blk0 = k_blk0 + n_kv
    cos, s1, s2 = _rope_tables(max(g.s, g.ss))
    n_prompt_tiles = g.t_prompt // tt

    def pos_map(i, kh):
        return (jnp.where(i < n_prompt_tiles, i % (g.s // tt), (i - n_prompt_tiles) % (g.ss // tt)), 0)

    scale = (hd ** -0.5) * LOG2E
    return pl.pallas_call(
        functools.partial(_qkprep_kernel, tq=tq, scale=scale),
        out_shape=(jax.ShapeDtypeStruct((n_kv, t // tq, GQA_GROUP, tq, hd), bf16),
                   jax.ShapeDtypeStruct((n_kv, t, hd), bf16),
                   jax.ShapeDtypeStruct((n_kv, t, hd), bf16)),
        grid=(t // tt, n_kv),
        in_specs=[pl.BlockSpec((tt, qw), lambda i, kh: (i, kh)),
                  pl.BlockSpec((tt, hd), lambda i, kh: (i, k_blk0 + kh)),
                  pl.BlockSpec((tt, hd), lambda i, kh: (i, v_blk0 + kh)),
                  pl.BlockSpec((1, hd), lambda i, kh: (0, 0)),
                  pl.BlockSpec((1, hd), lambda i, kh: (0, 0)),
                  pl.BlockSpec((tt, hd), pos_map),
                  pl.BlockSpec((tt, hd), pos_map),
                  pl.BlockSpec((tt, hd), pos_map)],
        out_specs=(pl.BlockSpec((1, n_sub, GQA_GROUP, tq, hd), lambda i, kh: (kh, i, 0, 0, 0)),
                   pl.BlockSpec((1, tt, hd), lambda i, kh: (kh, i, 0)),
                   pl.BlockSpec((1, tt, hd), lambda i, kh: (kh, i, 0))),
        compiler_params=_params(("parallel", "parallel"), 32 * 1024 * 1024),
        name="qk_prep",
    )(proj, proj, proj, g_q.reshape(1, hd), g_k.reshape(1, hd), cos, s1, s2)


def _attn_kernel(qt_ref, kb_ref, fi_ref, la_ref, q_ref, k_ref, v_ref, o_ref,
                 m_sc, l_sc, acc_sc, *, tq, tk):
    w = pl.program_id(1)
    hd = HEAD_DIM

    @pl.when(fi_ref[w] == 1)
    def _():
        m_sc[...] = jnp.full(m_sc.shape, -jnp.inf, f32)
        l_sc[...] = jnp.zeros(l_sc.shape, f32)
        acc_sc[...] = jnp.zeros(acc_sc.shape, f32)

    q = q_ref[0, 0].reshape(GQA_GROUP * tq, hd)
    for c in range(k_ref.shape[1] // tk):
        kc = k_ref[0, c * tk:(c + 1) * tk, :]
        vc = v_ref[0, c * tk:(c + 1) * tk, :]
        s = lax.dot_general(q, kc, (((1,), (1,)), ((), ())), preferred_element_type=f32)
        m_prev = m_sc[...]
        m_new = jnp.maximum(m_prev, jnp.max(s, axis=-1, keepdims=True))
        alpha = jnp.exp2(m_prev - m_new)
        p = jnp.exp2(s - m_new)
        l_sc[...] = alpha * l_sc[...] + jnp.sum(p, axis=-1, keepdims=True)
        acc_sc[...] = alpha * acc_sc[...] + jnp.dot(p.astype(bf16), vc, preferred_element_type=f32)
        m_sc[...] = m_new

    @pl.when(la_ref[w] == 1)
    def _():
        o = acc_sc[...] / l_sc[...]
        for gi in range(GQA_GROUP):
            o_ref[:, gi * hd:(gi + 1) * hd] = o[gi * tq:(gi + 1) * tq].astype(o_ref.dtype)


def _attention(g, q_r, k_r, v_r, n_kv, tq):
    hd = HEAD_DIM
    kvb = g.s
    tk = min(512, kvb)
    qt, kb, fi, la = [], [], [], []
    for tile in range(g.t // tq):
        r0 = tile * tq
        if r0 < g.t_prompt:
            blocks = [r0 // g.s]
        else:
            seq = (r0 - g.t_prompt) // g.ss
            base = g.n_prompt_chunks + seq * g.spc
            blocks = list(range(base, base + g.spc))
        for n, blk in enumerate(blocks):
            qt.append(tile)
            kb.append(blk)
            fi.append(int(n == 0))
            la.append(int(n == len(blocks) - 1))
    tabs = [jnp.asarray(np.asarray(v, np.int32)) for v in (qt, kb, fi, la)]
    rows = GQA_GROUP * tq
    grid_spec = pltpu.PrefetchScalarGridSpec(
        num_scalar_prefetch=4,
        grid=(n_kv, len(qt)),
        in_specs=[pl.BlockSpec((1, 1, GQA_GROUP, tq, hd), lambda kh, w, qt, kb, fi, la: (kh, qt[w], 0, 0, 0)),
                  pl.BlockSpec((1, kvb, hd), lambda kh, w, qt, kb, fi, la: (kh, kb[w], 0)),
                  pl.BlockSpec((1, kvb, hd), lambda kh, w, qt, kb, fi, la: (kh, kb[w], 0))],
        out_specs=pl.BlockSpec((tq, GQA_GROUP * hd), lambda kh, w, qt, kb, fi, la: (qt[w], kh)),
        scratch_shapes=[pltpu.VMEM((rows, 1), f32), pltpu.VMEM((rows, 1), f32),
                        pltpu.VMEM((rows, hd), f32)],
    )
    return pl.pallas_call(
        functools.partial(_attn_kernel, tq=tq, tk=tk),
        out_shape=jax.ShapeDtypeStruct((g.t, n_kv * GQA_GROUP * hd), bf16),
        grid_spec=grid_spec,
        compiler_params=_params(("parallel", "arbitrary"), 48 * 1024 * 1024),
        name="attention",
    )(*tabs, q_r, k_r, v_r)


def _edge_flags(g, r0, tm):
    r1 = r0 + tm
    start = jnp.where(r0 < g.t_prompt, lax.rem(r0, g.s) == 0, lax.rem(r0 - g.t_prompt, g.ss) == 0)
    end = jnp.where(r1 <= g.t_prompt, lax.rem(r1, g.s) == 0, lax.rem(r1 - g.t_prompt, g.ss) == 0)
    return jnp.where(start, 0.0, 1.0).astype(f32), jnp.where(end, 0.0, 1.0).astype(f32)


def _dwconv3(x, row_prev, row_next, w_ref, b_ref):
    tm = x.shape[0]
    rows = lax.broadcasted_iota(jnp.int32, x.shape, 0)
    x_prev = jnp.where(rows == 0, row_prev, pltpu.roll(x, 1, 0))
    x_next = jnp.where(rows == tm - 1, row_next, pltpu.roll(x, tm - 1, 0))
    return x_prev * w_ref[0:1, :] + x * w_ref[1:2, :] + x_next * w_ref[2:3, :] + b_ref[...]


def _halo_specs(tm, tc, n_rows, col_map):
    per = tm // BF16_ROWS
    last = n_rows // BF16_ROWS - 1
    return [pl.BlockSpec((BF16_ROWS, tc), lambda i, j: (jnp.maximum(i * per - 1, 0), col_map(j))),
            pl.BlockSpec((BF16_ROWS, tc), lambda i, j: (jnp.minimum((i + 1) * per, last), col_map(j)))]


def _merge_kernel(gb_ref, gc_ref, xin_ref, ga_ref, gv_ref, gcp_ref, gcn_ref, xp_ref, xn_ref,
                  attn_ref, w_ref, b_ref, o_ref, *, g, tm):
    keep_prev, keep_next = _edge_flags(g, pl.program_id(0) * tm, tm)
    last = BF16_ROWS - 1
    p = gc_ref[...].astype(f32) * xin_ref[...].astype(f32)
    row_prev = gcp_ref[last:last + 1, :].astype(f32) * xp_ref[last:last + 1, :].astype(f32) * keep_prev
    row_next = gcn_ref[0:1, :].astype(f32) * xn_ref[0:1, :].astype(f32) * keep_next
    conv = gb_ref[...].astype(f32) * _dwconv3(p, row_prev, row_next, w_ref, b_ref)
    merged = (jax.nn.sigmoid(ga_ref[...].astype(f32)) * attn_ref[...].astype(f32)
              + jax.nn.sigmoid(gv_ref[...].astype(f32)) * conv)
    o_ref[...] = merged.astype(o_ref.dtype)


def _merge(g, proj, attn, conv_w, conv_b, d, col0):
    tm = min(512, g.s)
    tc = min(512, d)
    nc = d // tc
    cb = [(col0 + n * d) // tc for n in range(5)]
    main = [pl.BlockSpec((tm, tc), (lambda i, j, o=o: (i, o + j))) for o in cb]
    halos = (_halo_specs(tm, tc, g.t, lambda j: cb[1] + j) + _halo_specs(tm, tc, g.t, lambda j: cb[2] + j))
    return pl.pallas_call(
        functools.partial(_merge_kernel, g=g, tm=tm),
        out_shape=jax.ShapeDtypeStruct((g.t, d), bf16),
        grid=(g.t // tm, nc),
        in_specs=main + halos + [pl.BlockSpec((tm, tc), lambda i, j: (i, j)),
                                 pl.BlockSpec((3, tc), lambda i, j: (0, j)),
                                 pl.BlockSpec((1, tc), lambda i, j: (0, j))],
        out_specs=pl.BlockSpec((tm, tc), lambda i, j: (i, j)),
        compiler_params=_params(("parallel", "parallel"), 32 * 1024 * 1024),
        name="merge",
    )(proj, proj, proj, proj, proj, proj, proj, proj, proj, attn, conv_w, conv_b.reshape(1, d))


def _post1_kernel(o_ref, xp_ref, xs_ref, gpost_ref, gpre_ref, gt_ref, sh_ref, sc_ref,
                  h_ref, u_ref, *, nb):
    c = pl.program_id(0)

    def compute(x):
        h = x + gt_ref[0] * (_rms(o_ref[...]) * gpost_ref[...])
        h_ref[...] = h
        u_ref[...] = ((_rms(h) * gpre_ref[...]) * (1.0 + sc_ref[0]) + sh_ref[0]).astype(u_ref.dtype)

    @pl.when(c < nb)
    def _():
        compute(xp_ref[0])

    @pl.when(c >= nb)
    def _():
        compute(xs_ref[0])


def _post1(g, out1, xp, xs, g_post, g_pre, mod3):
    d = xp.shape[-1]
    tm = min(256, g.s)
    nj = g.s // tm
    row = pl.BlockSpec((tm, d), lambda c, j: (c * nj + j, 0))
    vec = pl.BlockSpec((1, d), lambda c, j: (0, 0))
    return pl.pallas_call(
        functools.partial(_post1_kernel, nb=g.b),
        out_shape=(jax.ShapeDtypeStruct((g.t, d), f32), jax.ShapeDtypeStruct((g.t, d), bf16)),
        grid=(g.n_chunks, nj),
        in_specs=[row] + _x_specs(g, tm, d) + [vec, vec, _mod_spec(g, d, 2), _mod_spec(g, d, 3),
                                               _mod_spec(g, d, 4)],
        out_specs=(row, row),
        compiler_params=_params(("parallel", "parallel"), 48 * 1024 * 1024),
        name="post1",
    )(out1, xp, xs, g_post.reshape(1, d), g_pre.reshape(1, d), mod3, mod3, mod3)


def _post2_kernel(y_ref, h_ref, gpost_ref, gt_ref, o_ref):
    o_ref[0] = h_ref[...] + gt_ref[0] * (_rms(y_ref[...]) * gpost_ref[...])


def _post2(g, y, h1, g_post, mod3, sample):
    d = y.shape[-1]
    tm = min(256, g.s)
    if sample:
        nseq, slen, row0, mrow0 = g.bs, g.ss, g.t_prompt // tm, g.b
    else:
        nseq, slen, row0, mrow0 = g.b, g.s, 0, 0
    nj = slen // tm
    row = pl.BlockSpec((tm, d), lambda b, j: (row0 + b * nj + j, 0))
    return pl.pallas_call(
        _post2_kernel,
        out_shape=jax.ShapeDtypeStruct((nseq, slen, d), f32),
        grid=(nseq, nj),
        in_specs=[row, row, pl.BlockSpec((1, d), lambda b, j: (0, 0)),
                  pl.BlockSpec((1, 1, d), lambda b, j: (mrow0 + b, 0, 5))],
        out_specs=pl.BlockSpec((1, tm, d), lambda b, j: (b, j, 0)),
        compiler_params=_params(("parallel", "parallel"), 48 * 1024 * 1024),
        name="post2_sample" if sample else "post2_prompt",
    )(y, h1, g_post.reshape(1, d), mod3)


def _ffnact_kernel(za_ref, zb_ref, zap_ref, zan_ref, zbp_ref, zbn_ref, wa_ref, wb_ref, ba_ref, bb_ref,
                   o_ref, *, g, tm):
    keep_prev, keep_next = _edge_flags(g, pl.program_id(0) * tm, tm)
    last = BF16_ROWS - 1

    def conv(z_ref, zp_ref, zn_ref, w_ref, b_ref):
        return _dwconv3(z_ref[...].astype(f32),
                        zp_ref[last:last + 1, :].astype(f32) * keep_prev,
                        zn_ref[0:1, :].astype(f32) * keep_next, w_ref, b_ref)

    za = conv(za_ref, zap_ref, zan_ref, wa_ref, ba_ref)
    zb = conv(zb_ref, zbp_ref, zbn_ref, wb_ref, bb_ref)
    o_ref[...] = (za * jax.nn.sigmoid(za) * zb).astype(o_ref.dtype)


def _ffnact(g, z0, conv_w, conv_b, f):
    tm = g.s
    tc = 256
    assert f % tc == 0
    nc = f // tc
    wb = conv_b.reshape(1, 2 * f)
    return pl.pallas_call(
        functools.partial(_ffnact_kernel, g=g, tm=tm),
        out_shape=jax.ShapeDtypeStruct((g.t, f), bf16),
        grid=(g.t // tm, nc),
        in_specs=[pl.BlockSpec((tm, tc), lambda i, j: (i, j)),
                  pl.BlockSpec((tm, tc), lambda i, j: (i, nc + j))]
                 + _halo_specs(tm, tc, g.t, lambda j: j) + _halo_specs(tm, tc, g.t, lambda j: nc + j)
                 + [pl.BlockSpec((3, tc), lambda i, j: (0, j)),
                    pl.BlockSpec((3, tc), lambda i, j: (0, nc + j)),
                    pl.BlockSpec((1, tc), lambda i, j: (0, j)),
                    pl.BlockSpec((1, tc), lambda i, j: (0, nc + j))],
        out_specs=pl.BlockSpec((tm, tc), lambda i, j: (i, j)),
        compiler_params=_params(("parallel", "parallel"), 48 * 1024 * 1024),
        name="ffn_act",
    )(z0, z0, z0, z0, z0, z0, conv_w, conv_w, wb, wb)


def _layer(g, xp, xs, cp, cs, w_ada, b_ada, g_mix_pre, w_in, g_q, g_k, conv_w, conv_b, w_o, g_mix_post,
           g_ffn_pre, w_up, ffn_conv_w, ffn_conv_b, w_down, g_ffn_post):
    d = xp.shape[-1]
    f = w_down.shape[0]
    n_kv = d // (HEAD_DIM * GQA_GROUP)
    kv_dim = n_kv * HEAD_DIM
    tq = min(256, g.s)
    tm = min(1024, g.s)

    n_mod = g.b + g.bs
    c_pad = jnp.concatenate([cp, cs, jnp.zeros((-n_mod % 8, d), f32)], axis=0)
    mod3 = _ada(c_pad, w_ada, b_ada).reshape(c_pad.shape[0], 1, 6 * d)

    u = _prenorm(g, xp, xs, g_mix_pre, mod3, 0, 1)
    proj = _matmul(u, w_in.astype(bf16), bf16, tm, min(1024, d), "in_proj")
    q_r, k_r, v_r = _qkprep(g, proj, g_q, g_k, n_kv, tq)
    attn = _attention(g, q_r, k_r, v_r, n_kv, tq)
    merged = _merge(g, proj, attn, conv_w, conv_b, d, d + 2 * kv_dim)
    out1 = _matmul(merged, w_o.astype(bf16), f32, tm, min(1024, d), "out_proj")
    h1, u2 = _post1(g, out1, xp, xs, g_mix_post, g_ffn_pre, mod3)
    z0 = _matmul(u2, w_up.astype(bf16), bf16, tm, 512 if (2 * f) % 512 == 0 else 256, "ffn_up")
    act = _ffnact(g, z0, ffn_conv_w, ffn_conv_b, f)
    y = _matmul(act, w_down.astype(bf16), f32, min(512, g.s), min(512, d), "ffn_down")
    yp = _post2(g, y, h1, g_ffn_post, mod3, sample=False)
    ys = _post2(g, y, h1, g_ffn_post, mod3, sample=True)
    return yp, ys


def kernel(x_prompt, x_sample, c_prompt, c_sample, w_ada, b_ada, g_mix_pre, w_in, g_q, g_k, conv_w, conv_b,
           w_o, g_mix_post, g_ffn_pre, w_up, ffn_conv_w, ffn_conv_b, w_down, g_ffn_post):
    b, s, _ = x_prompt.shape
    bs, ss, _ = x_sample.shape
    g = _Geom(b, s, bs, ss)
    hp, hs = x_prompt, x_sample
    for l in range(w_ada.shape[0]):
        hp, hs = _layer(g, hp, hs, c_prompt, c_sample, w_ada[l], b_ada[l], g_mix_pre[l], w_in[l], g_q[l],
                        g_k[l], conv_w[l], conv_b[l], w_o[l], g_mix_post[l], g_ffn_pre[l], w_up[l],
                        ffn_conv_w[l], ffn_conv_b[l], w_down[l], g_ffn_post[l])
    return (hp, hs)
```
